```python
import math
import jax, jax.numpy as jnp
from jax import lax
import numpy as np

D_MODEL = 2048
BATCH = 16
SEQ = 2048
DEPTH = 1

CTX_LEN = 256
GRID_W = 64
D_NA = D_MODEL // 2
HEAD_DIM_NA = 128
N_HEADS_NA = D_NA // HEAD_DIM_NA
WIN_ROWS = 8
WIN_COLS = 16
ROPE_BASE = 10000.0
D_SSM = D_MODEL // 2
SSM_HEAD_DIM = 64
N_HEADS_SSM = D_SSM // SSM_HEAD_DIM
SSM_GROUPS = 2
D_STATE = 128
D_CONV = 5
CHUNK = 128
CONV_DIM = D_SSM + 2 * SSM_GROUPS * D_STATE
D_MIX = D_NA + D_SSM
D_IN_PROJ = 3 * D_NA + D_SSM + CONV_DIM + 2 * N_HEADS_SSM
N_EXPERTS = 16
CAPACITY_FACTOR = 2
D_FF_EXPERT = 2816
EPS = 1e-6

kernel_name = "hybrid_na_ssd_ec_dit_block"

F32 = jnp.float32


def rmsnorm(x, w):
    xf = x.astype(F32)
    y = xf * lax.rsqrt(jnp.mean(xf * xf, axis=-1, keepdims=True) + EPS)
    return (y * w.astype(F32)).astype(x.dtype)


def ada_modulation(cond, w, b):
    mod = jax.nn.silu(cond) @ w + b
    return jnp.split(mod[..., None, :], 6, axis=-1)


def modulate(h, shift, scale):
    return h * (1.0 + scale) + shift


def split_in_proj(p):
    cuts = [D_NA, 2 * D_NA, 3 * D_NA, 3 * D_NA + D_SSM, 3 * D_NA + D_SSM + CONV_DIM]
    return jnp.split(p, cuts, axis=-1)


def axial_rope(x, row_pos, col_pos):
    half = x.shape[-1] // 2
    quarter = half // 2
    freqs = ROPE_BASE ** (-jnp.arange(quarter, dtype=F32) / quarter)

    def rot(u, pos):
        ang = pos.astype(F32)[:, None] * freqs
        cos = jnp.cos(ang)[None, :, None, :]
        sin = jnp.sin(ang)[None, :, None, :]
        u1, u2 = u[..., :quarter], u[..., quarter:]
        return jnp.concatenate([u1 * cos - u2 * sin, u2 * cos + u1 * sin], axis=-1)

    xf = x.astype(F32)
    out = jnp.concatenate([rot(xf[..., :half], row_pos), rot(xf[..., half:], col_pos)], axis=-1)
    return out.astype(x.dtype)


def neighbourhood_attention(q, k, v, k_ctx, v_ctx, rpb):
    nb, t, nh, dh = q.shape
    rows = t // GRID_W
    kh = min(WIN_ROWS, rows)
    kw = min(WIN_COLS, GRID_W)
    scale = dh ** -0.5
    qg = q.reshape(nb, rows, GRID_W, nh, dh)
    kg = k.reshape(nb, rows, GRID_W, nh, dh)
    vg = v.reshape(nb, rows, GRID_W, nh, dh)
    col = jnp.arange(GRID_W)
    col_start = jnp.clip(col - kw // 2, 0, GRID_W - kw)
    col_in = (col[None, :] >= col_start[:, None]) & (col[None, :] < col_start[:, None] + kw)
    col_idx = jnp.clip(col[None, :] - col[:, None] + WIN_COLS - 1, 0, 2 * WIN_COLS - 2)
    rpb_cols = rpb.astype(F32)[:, :, col_idx]

    def row_block(r):
        rs = jnp.clip(r - kh // 2, 0, rows - kh)
        q_r = lax.dynamic_index_in_dim(qg, r, axis=1, keepdims=False)
        k_r = lax.dynamic_slice_in_dim(kg, rs, kh, axis=1)
        v_r = lax.dynamic_slice_in_dim(vg, rs, kh, axis=1)
        row_idx = rs + jnp.arange(kh) - r + WIN_ROWS - 1
        bias = jnp.take(rpb_cols, row_idx, axis=1).transpose(0, 2, 1, 3)
        s_loc = jnp.einsum('bqhd,bkwhd->bhqkw', q_r, k_r).astype(F32) * scale + bias[None]
        s_loc = jnp.where(col_in[:, None, :], s_loc, -jnp.inf)
        s_ctx = jnp.einsum('bqhd,blhd->bhql', q_r, k_ctx).astype(F32) * scale
        s = jnp.concatenate([s_loc.reshape(nb, nh, GRID_W, kh * GRID_W), s_ctx], axis=-1)
        p = jax.nn.softmax(s, axis=-1).astype(v.dtype)
        p_loc = p[..., :kh * GRID_W].reshape(nb, nh, GRID_W, kh, GRID_W)
        p_ctx = p[..., kh * GRID_W:]
        return (jnp.einsum('bhqkw,bkwhd->bqhd', p_loc, v_r)
                + jnp.einsum('bhql,blhd->bqhd', p_ctx, v_ctx))

    out = lax.map(row_block, jnp.arange(rows))
    return jnp.moveaxis(out, 0, 1).reshape(nb, t, nh * dh)


def context_attention(q, k, v):
    nb, l, nh, dh = q.shape
    s = jnp.einsum('bqhd,bkhd->bhqk', q, k).astype(F32) * dh ** -0.5
    p = jax.nn.softmax(s, axis=-1).astype(v.dtype)
    return jnp.einsum('bhqk,bkhd->bqhd', p, v).reshape(nb, l, nh * dh)


def depthwise_conv(u, w, b):
    out = lax.conv_general_dilated(
        u, w.astype(u.dtype)[:, None, :], window_strides=(1,),
        padding=[(D_CONV // 2, D_CONV // 2)],
        dimension_numbers=('NWC', 'WIO', 'NWC'), feature_group_count=u.shape[-1])
    return out + b.astype(u.dtype)


def ssm_inputs(xbc, dt_raw, conv_w, conv_b, dt_bias):
    nb, t = xbc.shape[:2]
    xbc = jax.nn.silu(depthwise_conv(xbc, conv_w, conv_b))
    xs, bm, cm = jnp.split(xbc, [D_SSM, D_SSM + SSM_GROUPS * D_STATE], axis=-1)
    xs = xs.reshape(nb, t, N_HEADS_SSM, SSM_HEAD_DIM)
    bm = bm.reshape(nb, t, SSM_GROUPS, D_STATE)
    cm = cm.reshape(nb, t, SSM_GROUPS, D_STATE)
    dt = jax.nn.softplus(dt_raw.reshape(nb, t, 2, N_HEADS_SSM).astype(F32) + dt_bias.astype(F32))
    return xs, dt, bm, cm


def ssd_chunked(x, dt, A, bm, cm, h0):
    nb, t, nh, p = x.shape
    g, n = bm.shape[-2:]
    e = nh // g
    nc = t // CHUNK
    xf = x.astype(F32).reshape(nb, nc, CHUNK, g, e, p)
    dtc = dt.astype(F32).reshape(nb, nc, CHUNK, g, e)
    a_cs = jnp.cumsum(dtc * A.astype(F32).reshape(g, e), axis=2)
    xdt = xf * dtc[..., None]
    bc = bm.astype(F32).reshape(nb, nc, CHUNK, g, n)
    cc = cm.astype(F32).reshape(nb, nc, CHUNK, g, n)
    lower = jnp.tril(jnp.ones((CHUNK, CHUNK), dtype=bool))
    seg = a_cs[:, :, :, None] - a_cs[:, :, None, :]
    decay_ls = jnp.exp(jnp.where(lower[:, :, None, None], seg, -jnp.inf))
    cb = jnp.einsum('bclgn,bcsgn->bclsg', cc, bc)
    y_diag = jnp.einsum('bclsg,bclsge,bcsgep->bclgep', cb, decay_ls, xdt)
    decay_end = jnp.exp(a_cs[:, :, -1:] - a_cs)
    chunk_states = jnp.einsum('bclgn,bclge,bclgep->bcgepn', bc, decay_end, xdt)
    chunk_decay = jnp.exp(a_cs[:, :, -1])

    def step(h, inp):
        dec, st = inp
        return dec[..., None, None] * h + st, h

    h_last, h_prev = lax.scan(step, h0, (jnp.moveaxis(chunk_decay, 1, 0), jnp.moveaxis(chunk_states, 1, 0)))
    h_prev = jnp.moveaxis(h_prev, 0, 1)
    y_off = jnp.einsum('bclgn,bcgepn,bclge->bclgep', cc, h_prev, jnp.exp(a_cs))
    return (y_diag + y_off).reshape(nb, t, nh, p), h_last


def bidir_ssd(xs, dt, bm, cm, a_log, d_skip, h0_f, h0_b):
    nb, t = xs.shape[:2]
    A = -jnp.exp(a_log.astype(F32))
    flip = lambda u: jnp.flip(u, axis=1)
    y_f, h_f = ssd_chunked(xs, dt[:, :, 0], A[0], bm, cm, h0_f)
    y_b, h_b = ssd_chunked(flip(xs), flip(dt[:, :, 1]), A[1], flip(bm), flip(cm), h0_b)
    y = y_f + flip(y_b) + d_skip.astype(F32)[:, None] * xs.astype(F32)
    return y.astype(xs.dtype).reshape(nb, t, D_SSM), h_f, h_b


def gated_group_rmsnorm(y, z, w):
    nb, t = y.shape[:2]
    u = (y * jax.nn.silu(z)).astype(F32).reshape(nb, t, SSM_GROUPS, D_SSM // SSM_GROUPS)
    u = u * lax.rsqrt(jnp.mean(u * u, axis=-1, keepdims=True) + EPS)
    return (u.reshape(nb, t, D_SSM) * w.astype(F32)).astype(y.dtype)


def expert_choice_ffn(h, w_router, w_gate, w_up, w_down):
    nb, t, d = h.shape
    cap = max(1, CAPACITY_FACTOR * t // N_EXPERTS)
    aff = jax.nn.softmax(jnp.einsum('btd,de->bte', h, w_router).astype(F32), axis=-1)
    gates, idx = lax.top_k(jnp.swapaxes(aff, 1, 2), cap)
    bi = jnp.arange(nb)[:, None, None]
    xs = h[bi, idx]
    hid = jax.nn.silu(jnp.einsum('becd,edf->becf', xs, w_gate)) * jnp.einsum('becd,edf->becf', xs, w_up)
    out = jnp.einsum('becf,efd->becd', hid, w_down) * gates[..., None].astype(h.dtype)
    return jnp.zeros_like(h).at[bi, idx].add(out)


def setup_inputs(seed: int = 0) -> dict:
    key = jax.random.key(seed)
    ks = jax.random.split(key, 24)

    def nrm(k, shape, scale):
        return jax.random.normal(k, shape, F32) * scale

    dt_init = jnp.exp(jax.random.uniform(ks[12], (DEPTH, 2, N_HEADS_SSM), F32,
                                         minval=math.log(1e-3), maxval=math.log(1e-1)))
    return {
        "x": nrm(ks[0], (BATCH, SEQ, D_MODEL), 1.0),
        "c": nrm(ks[1], (BATCH, D_MODEL), 1.0),
        "ctx": nrm(ks[2], (BATCH, CTX_LEN, D_MODEL), 1.0),
        "c_ctx": nrm(ks[3], (D_MODEL,), 1.0),
        "w_ada": nrm(ks[4], (DEPTH, D_MODEL, 6 * D_MODEL), D_MODEL ** -0.5),
        "b_ada": nrm(ks[5], (DEPTH, 6 * D_MODEL), 0.02),
        "norm_mix_w": 1.0 + nrm(ks[6], (DEPTH, D_MODEL), 0.02),
        "norm_ffn_w": 1.0 + nrm(ks[7], (DEPTH, D_MODEL), 0.02),
        "w_in": nrm(ks[8], (DEPTH, D_MODEL, D_IN_PROJ), D_MODEL ** -0.5),
        "rpb": nrm(ks[9], (DEPTH, N_HEADS_NA, 2 * WIN_ROWS - 1, 2 * WIN_COLS - 1), 0.1),
        "conv_w": nrm(ks[10], (DEPTH, D_CONV, CONV_DIM), D_CONV ** -0.5),
        "conv_b": nrm(ks[11], (DEPTH, CONV_DIM), 0.02),
        "dt_bias": dt_init + jnp.log(-jnp.expm1(-dt_init)),
        "a_log": jnp.log(jax.random.uniform(ks[13], (DEPTH, 2, N_HEADS_SSM), F32, minval=1.0, maxval=16.0)),
        "d_skip": 1.0 + nrm(ks[14], (DEPTH, N_HEADS_SSM), 0.02),
        "ssm_norm_w": 1.0 + nrm(ks[15], (DEPTH, D_SSM), 0.02),
        "w_out": nrm(ks[16], (DEPTH, D_MIX, D_MODEL), D_MIX ** -0.5),
        "w_router": nrm(ks[17], (DEPTH, D_MODEL, N_EXPERTS), D_MODEL ** -0.5),
        "w_gate": nrm(ks[18], (DEPTH, N_EXPERTS, D_MODEL, D_FF_EXPERT), D_MODEL ** -0.5),
        "w_up": nrm(ks[19], (DEPTH, N_EXPERTS, D_MODEL, D_FF_EXPERT), D_MODEL ** -0.5),
        "w_down": nrm(ks[20], (DEPTH, N_EXPERTS, D_FF_EXPERT, D_MODEL), D_FF_EXPERT ** -0.5),
        "final_norm_w": 1.0 + nrm(ks[21], (D_MODEL,), 0.02),
    }


def reference(x, c, ctx, c_ctx, w_ada, b_ada, norm_mix_w, norm_ffn_w, w_in, rpb, conv_w, conv_b,
              dt_bias, a_log, d_skip, ssm_norm_w, w_out, w_router, w_gate, w_up, w_down, final_norm_w):
    nb, t, _ = x.shape
    l = ctx.shape[1]
    pos = jnp.arange(t)
    row_pos, col_pos = pos // GRID_W, pos % GRID_W
    h_lat, h_ctx = x, ctx
    for i in range(DEPTH):
        sh1, sc1, g1, sh2, sc2, g2 = ada_modulation(c, w_ada[i], b_ada[i])
        csh1, csc1, cg1, csh2, csc2, cg2 = ada_modulation(c_ctx, w_ada[i], b_ada[i])

        u_lat = modulate(rmsnorm(h_lat, norm_mix_w[i]), sh1, sc1)
        u_ctx = modulate(rmsnorm(h_ctx, norm_mix_w[i]), csh1, csc1)
        q, k, v, z, xbc, dt_raw = split_in_proj(u_lat @ w_in[i])
        qc, kc, vc, zc, xbcc, dtc_raw = split_in_proj(u_ctx @ w_in[i])

        q = axial_rope(q.reshape(nb, t, N_HEADS_NA, HEAD_DIM_NA), row_pos, col_pos)
        k = axial_rope(k.reshape(nb, t, N_HEADS_NA, HEAD_DIM_NA), row_pos, col_pos)
        v = v.reshape(nb, t, N_HEADS_NA, HEAD_DIM_NA)
        kc = kc.reshape(nb, l, N_HEADS_NA, HEAD_DIM_NA)
        vc = vc.reshape(nb, l, N_HEADS_NA, HEAD_DIM_NA)
        attn_lat = neighbourhood_attention(q, k, v, kc, vc, rpb[i])

        h0 = jnp.zeros((nb, SSM_GROUPS, N_HEADS_SSM // SSM_GROUPS, SSM_HEAD_DIM, D_STATE), F32)
        xs_c, dt_c, bm_c, cm_c = ssm_inputs(xbcc, dtc_raw, conv_w[i], conv_b[i], dt_bias[i])
        y_c, hf_c, hb_c = bidir_ssd(xs_c, dt_c, bm_c, cm_c, a_log[i], d_skip[i], h0, h0)
        xs_l, dt_l, bm_l, cm_l = ssm_inputs(xbc, dt_raw, conv_w[i], conv_b[i], dt_bias[i])
        y_l, _, _ = bidir_ssd(xs_l, dt_l, bm_l, cm_l, a_log[i], d_skip[i], hf_c, hb_c)
        ssm_lat = gated_group_rmsnorm(y_l, z, ssm_norm_w[i])

        mix_lat = jnp.concatenate([attn_lat, ssm_lat], axis=-1) @ w_out[i]
        h_lat = h_lat + g1 * mix_lat

        u2 = modulate(rmsnorm(h_lat, norm_ffn_w[i]), sh2, sc2)
        h_lat = h_lat + g2 * expert_choice_ffn(u2, w_router[i], w_gate[i], w_up[i], w_down[i])

        if i < DEPTH - 1:
            attn_ctx = context_attention(qc.reshape(nb, l, N_HEADS_NA, HEAD_DIM_NA), kc, vc)
            ssm_ctx = gated_group_rmsnorm(y_c, zc, ssm_norm_w[i])
            h_ctx = h_ctx + cg1 * (jnp.concatenate([attn_ctx, ssm_ctx], axis=-1) @ w_out[i])
            u2c = modulate(rmsnorm(h_ctx, norm_ffn_w[i]), csh2, csc2)
            h_ctx = h_ctx + cg2 * expert_choice_ffn(u2c, w_router[i], w_gate[i], w_up[i], w_down[i])

    return rmsnorm(h_lat, final_norm_w)
```

```python
import functools
import math

import jax
import jax.numpy as jnp
from jax import lax
from jax.experimental import pallas as pl
from jax.experimental.pallas import tpu as pltpu

F32 = jnp.float32
BF16 = jnp.bfloat16

EPS = 1e-6
GRID_W = 64
HEAD_DIM_NA = 128
WIN_ROWS = 8
WIN_COLS = 16
ROPE_BASE = 10000.0
SSM_HEAD_DIM = 64
SSM_GROUPS = 2
D_STATE = 128
D_CONV = 5
CHUNK = 128
CAPACITY_FACTOR = 2
LANES = 128
NEG_BIG = -1e30

_MIB = 1024 * 1024


def _params(semantics, vmem_mib):
    return pltpu.CompilerParams(dimension_semantics=semantics, vmem_limit_bytes=vmem_mib * _MIB)


def _silu(v):
    return v / (1.0 + jnp.exp(-v))


def _dot(a, b):
    return jnp.dot(a, b, preferred_element_type=F32)


def _dot_nt(a, b):
    return lax.dot_general(a, b, (((1,), (1,)), ((), ())), preferred_element_type=F32)


def _split3(v):
    p1 = v.astype(BF16)
    r1 = v - p1.astype(F32)
    p2 = r1.astype(BF16)
    p3 = (r1 - p2.astype(F32)).astype(BF16)
    return p1, p2, p3


def _split2(v):
    p1 = v.astype(BF16)
    p2 = (v - p1.astype(F32)).astype(BF16)
    return p1, p2


def _ada_kernel(c_ref, w_ref, b_ref, o_ref):
    s = _silu(c_ref[...])
    o_ref[...] = _dot(s.astype(BF16), w_ref[...].astype(BF16)) + b_ref[...]


def _ada_modulation(cond, w, b):
    m, d = cond.shape
    n = w.shape[1]
    tn = 1024
    return pl.pallas_call(
        _ada_kernel,
        grid=(n // tn,),
        in_specs=[
            pl.BlockSpec((m, d), lambda j: (0, 0)),
            pl.BlockSpec((d, tn), lambda j: (0, j)),
            pl.BlockSpec((1, tn), lambda j: (0, j)),
        ],
        out_specs=pl.BlockSpec((m, tn), lambda j: (0, j)),
        out_shape=jax.ShapeDtypeStruct((m, n), F32),
        compiler_params=_params(("arbitrary",), 40),
    )(cond, w, b.reshape(1, n))


def _inproj_kernel(x_ref, nw_ref, sh_ref, sc_ref, cos_ref, s1_ref, s2_ref, wm_ref, wd_ref,
                   qkv_ref, zx_ref, dt_ref, xn_ref, *, n_rope, n_qkv, n_main):
    j = pl.program_id(2)

    @pl.when(j == 0)
    def _():
        x = x_ref[0]
        y = x * lax.rsqrt(jnp.mean(x * x, axis=-1, keepdims=True) + EPS) * nw_ref[...]
        xn_ref[...] = (y * (1.0 + sc_ref[0]) + sh_ref[0]).astype(BF16)

    if n_rope > 0:
        @pl.when(j < n_rope)
        def _():
            acc = _dot(xn_ref[...], wm_ref[...])
            cos, s1, s2 = cos_ref[...], s1_ref[...], s2_ref[...]
            for hh in range(acc.shape[1] // LANES):
                a = acc[:, hh * LANES:(hh + 1) * LANES]
                r = a * cos + pltpu.roll(a, 96, 1) * s1 + pltpu.roll(a, 32, 1) * s2
                qkv_ref[0, :, hh * LANES:(hh + 1) * LANES] = r.astype(BF16)

    @pl.when((j >= n_rope) & (j < n_qkv))
    def _():
        qkv_ref[0] = _dot(xn_ref[...], wm_ref[...]).astype(BF16)

    @pl.when((j >= n_qkv) & (j < n_main))
    def _():
        zx_ref[0] = _dot(xn_ref[...], wm_ref[...])

    @pl.when(j == n_main)
    def _():
        dt_ref[0] = _dot(xn_ref[...], wd_ref[...])


def _in_proj(x, norm_w, shift, scale, rope_tabs, w_main, w_dt, *, d_na, rope, tm):
    b, t, d = x.shape
    tn = 512
    n_main = w_main.shape[1] // tn
    n_qkv = 3 * d_na // tn
    n_zx = n_main - n_qkv
    n_rope = (2 * d_na // tn) if rope else 0
    ndt = w_dt.shape[1]
    cos, s1, s2 = rope_tabs
    kern = functools.partial(_inproj_kernel, n_rope=n_rope, n_qkv=n_qkv, n_main=n_main)
    return pl.pallas_call(
        kern,
        grid=(b, t // tm, n_main + 1),
        in_specs=[
            pl.BlockSpec((1, tm, d), lambda bi, i, j: (bi, i, 0)),
            pl.BlockSpec((1, d), lambda bi, i, j: (0, 0)),
            pl.BlockSpec((1, 1, d), lambda bi, i, j: (bi, 0, 0)),
            pl.BlockSpec((1, 1, d), lambda bi, i, j: (bi, 0, 0)),
            pl.BlockSpec((tm, LANES), lambda bi, i, j: (i, 0)),
            pl.BlockSpec((tm, LANES), lambda bi, i, j: (i, 0)),
            pl.BlockSpec((tm, LANES), lambda bi, i, j: (i, 0)),
            pl.BlockSpec((d, tn), lambda bi, i, j: (0, jnp.minimum(j, n_main - 1))),
            pl.BlockSpec((d, ndt), lambda bi, i, j: (0, 0)),
        ],
        out_specs=[
            pl.BlockSpec((1, tm, tn), lambda bi, i, j: (bi, i, jnp.minimum(j, n_qkv - 1))),
            pl.BlockSpec((1, tm, tn), lambda bi, i, j: (bi, i, jnp.clip(j - n_qkv, 0, n_zx - 1))),
            pl.BlockSpec((1, tm, ndt), lambda bi, i, j: (bi, i, 0)),
        ],
        out_shape=[
            jax.ShapeDtypeStruct((b, t, n_qkv * tn), BF16),
            jax.ShapeDtypeStruct((b, t, n_zx * tn), F32),
            jax.ShapeDtypeStruct((b, t, ndt), F32),
        ],
        scratch_shapes=[pltpu.VMEM((tm, d), BF16)],
        compiler_params=_params(("arbitrary", "arbitrary", "arbitrary"), 48),
    )(x, norm_w.reshape(1, d), shift, scale, cos, s1, s2, w_main, w_dt)


def _rope_tables(t):
    quarter = HEAD_DIM_NA // 4
    pos = jnp.arange(t)
    freqs = ROPE_BASE ** (-jnp.arange(quarter, dtype=F32) / quarter)
    ang_r = (pos // GRID_W).astype(F32)[:, None] * freqs
    ang_c = (pos % GRID_W).astype(F32)[:, None] * freqs
    cr, sr, cc, sn = jnp.cos(ang_r), jnp.sin(ang_r), jnp.cos(ang_c), jnp.sin(ang_c)
    zero = jnp.zeros_like(sr)
    cos = jnp.concatenate([cr, cr, cc, cc], axis=-1)
    s1 = jnp.concatenate([-sr, zero, -sn, zero], axis=-1)
    s2 = jnp.concatenate([zero, sr, zero, sn], axis=-1)
    return cos, s1, s2


def _attn_kernel(q_ref, k_ref, v_ref, kc_ref, vc_ref, bias_ref, o_ref, *, rows, scale):
    kc = kc_ref[0]
    vc = vc_ref[0]
    win = WIN_ROWS * GRID_W

    def body(r, carry):
        rs = jnp.clip(r - WIN_ROWS // 2, 0, rows - WIN_ROWS)
        q = q_ref[0, pl.ds(pl.multiple_of(r * GRID_W, GRID_W), GRID_W), :]
        k0 = pl.multiple_of(rs * GRID_W, GRID_W)
        kw = k_ref[0, pl.ds(k0, win), :]
        vw = v_ref[0, pl.ds(k0, win), :]
        s_loc = _dot_nt(q, kw) * scale + bias_ref[0, rs - r + WIN_ROWS - 1]
        s_ctx = _dot_nt(q, kc) * scale
        m = jnp.maximum(jnp.max(s_loc, axis=-1, keepdims=True), jnp.max(s_ctx, axis=-1, keepdims=True))
        p_loc = jnp.exp(s_loc - m)
        p_ctx = jnp.exp(s_ctx - m)
        inv = 1.0 / (jnp.sum(p_loc, axis=-1, keepdims=True) + jnp.sum(p_ctx, axis=-1, keepdims=True))
        o = _dot((p_loc * inv).astype(BF16), vw) + _dot((p_ctx * inv).astype(BF16), vc)
        o_ref[0, pl.ds(pl.multiple_of(r * GRID_W, GRID_W), GRID_W), :] = o.astype(BF16)
        return carry

    lax.fori_loop(0, rows, body, 0)


def _attention_bias(rpb):
    col = jnp.arange(GRID_W)
    col_start = jnp.clip(col - WIN_COLS // 2, 0, GRID_W - WIN_COLS)
    col_in = (col[None, :] >= col_start[:, None]) & (col[None, :] < col_start[:, None] + WIN_COLS)
    col_idx = jnp.clip(col[None, :] - col[:, None] + WIN_COLS - 1, 0, 2 * WIN_COLS - 2)
    masked = jnp.where(col_in[None, None], rpb.astype(F32)[:, :, col_idx], NEG_BIG)
    idx = jnp.arange(WIN_ROWS)[:, None] + jnp.arange(WIN_ROWS)[None, :]
    tab = masked[:, idx]
    tab = jnp.transpose(tab, (0, 1, 3, 2, 4))
    return tab.reshape(rpb.shape[0], WIN_ROWS, GRID_W, WIN_ROWS * GRID_W)


def _attention(qkv, qkv_c, bias_tab, *, nh):
    b, t, _ = qkv.shape
    l = qkv_c.shape[1]
    rows = t // GRID_W
    assert rows >= WIN_ROWS and GRID_W >= WIN_COLS
    kern = functools.partial(_attn_kernel, rows=rows, scale=HEAD_DIM_NA ** -0.5)
    return pl.pallas_call(
        kern,
        grid=(nh, b),
        in_specs=[
            pl.BlockSpec((1, t, LANES), lambda h, bi: (bi, 0, h)),
            pl.BlockSpec((1, t, LANES), lambda h, bi: (bi, 0, nh + h)),
            pl.BlockSpec((1, t, LANES), lambda h, bi: (bi, 0, 2 * nh + h)),
            pl.BlockSpec((1, l, LANES), lambda h, bi: (bi, 0, nh + h)),
            pl.BlockSpec((1, l, LANES), lambda h, bi: (bi, 0, 2 * nh + h)),
            pl.BlockSpec((1, WIN_ROWS, GRID_W, WIN_ROWS * GRID_W), lambda h, bi: (h, 0, 0, 0)),
        ],
        out_specs=pl.BlockSpec((1, t, LANES), lambda h, bi: (bi, 0, h)),
        out_shape=jax.ShapeDtypeStruct((b, t, nh * LANES), BF16),
        compiler_params=_params(("arbitrary", "arbitrary"), 32),
    )(qkv, qkv, qkv, qkv_c, qkv_c, bias_tab)


def _conv_silu(u_ref, w_ref, b_ref, c, nc):
    c = jnp.asarray(c, jnp.int32)
    base = pl.multiple_of(c * CHUNK, CHUNK)
    cur = u_ref[0, pl.ds(base, CHUNK), :]
    p0 = pl.multiple_of(jnp.maximum(base - 8, 0), 8)
    n0 = pl.multiple_of(jnp.minimum(base + CHUNK, nc * CHUNK - 8), 8)
    prev = u_ref[0, pl.ds(p0, 8), :] * (c > 0).astype(F32)
    nxt = u_ref[0, pl.ds(n0, 8), :] * (c < nc - 1).astype(F32)
    w = jnp.concatenate([prev, cur, nxt], axis=0)
    acc = jnp.zeros_like(cur) + b_ref[...]
    off = 8 - D_CONV // 2
    for j in range(D_CONV):
        acc = acc + w[off + j:off + j + CHUNK, :] * w_ref[j:j + 1, :]
    return _silu(acc)


def _ssd_chunk(xs_ref, b_ref, c_ref, dt_ref, cw_refs, cb_refs, dtb_ref, alog_ref, consts, c, nc, want_y):
    tri, expand = consts
    hp = xs_ref.shape[-1]
    heads = hp // SSM_HEAD_DIM
    xs = _conv_silu(xs_ref, cw_refs[0], cb_refs[0], c, nc)
    bm = _conv_silu(b_ref, cw_refs[1], cb_refs[1], c, nc)
    base = pl.multiple_of(jnp.asarray(c, jnp.int32) * CHUNK, CHUNK)
    raw = dt_ref[0, pl.ds(base, CHUNK), :] + dtb_ref[0]
    dt = jnp.maximum(raw, 0.0) + jnp.log1p(jnp.exp(-jnp.abs(raw)))
    a_neg = -jnp.exp(alog_ref[0])
    lane = lax.broadcasted_iota(jnp.int32, (CHUNK, LANES), 1)
    fwd_lane = lane < heads
    d1, d2, d3 = _split3(dt * a_neg)
    cs2 = _dot(tri, d1) + _dot(tri, d2) + _dot(tri, d3)
    acs = jnp.where(fwd_lane, cs2[:CHUNK], cs2[CHUNK:])
    tot = jnp.where(fwd_lane[:1], acs[CHUNK - 1:CHUNK], acs[0:1])
    ea = jnp.exp(acs)
    wgt = jnp.exp(tot - acs) * dt
    dec = jnp.broadcast_to(jnp.exp(tot), (8, LANES))
    st = jnp.concatenate([wgt, ea, dec], axis=0)
    e1, e2 = _split2(st)
    ex = _dot(e1, expand) + _dot(e2, expand)
    w_f, w_b = ex[:CHUNK, :hp], ex[:CHUNK, hp:]
    ea_f, ea_b = ex[CHUNK:2 * CHUNK, :hp], ex[CHUNK:2 * CHUNK, hp:]
    dec_f, dec_b = ex[2 * CHUNK:2 * CHUNK + 1, :hp], ex[2 * CHUNK:2 * CHUNK + 1, hp:]
    bt = bm.T.astype(BF16)
    xw = jnp.concatenate([(xs * w_f).astype(BF16), (xs * w_b).astype(BF16)], axis=1)
    states = _dot(bt, xw)
    out = dict(xs=xs, st_f=states[:, :hp], st_b=states[:, hp:], dec_f=dec_f, dec_b=dec_b, ea_f=ea_f, ea_b=ea_b)
    if not want_y:
        return out
    cm = _conv_silu(c_ref, cw_refs[2], cb_refs[2], c, nc)
    cmb = cm.astype(BF16)
    cb = _dot_nt(cmb, bm.astype(BF16))
    acs_t = acs.T
    dt_t = dt.T
    li = lax.broadcasted_iota(jnp.int32, (CHUNK, CHUNK), 0)
    si = lax.broadcasted_iota(jnp.int32, (CHUNK, CHUNK), 1)
    low, up = li >= si, li <= si
    xsb = xs.astype(BF16)
    lane_x = lax.broadcasted_iota(jnp.int32, (CHUNK, LANES), 1)
    ys = []
    for blk in range(hp // LANES):
        xblk = xsb[:, blk * LANES:(blk + 1) * LANES]
        halves = (jnp.where(lane_x < SSM_HEAD_DIM, xblk, jnp.zeros_like(xblk)),
                  jnp.where(lane_x >= SSM_HEAD_DIM, xblk, jnp.zeros_like(xblk)))
        yb = None
        for half in range(2):
            e = 2 * blk + half
            seg_f = acs[:, e:e + 1] - acs_t[e:e + 1, :]
            seg_b = acs[:, heads + e:heads + e + 1] - acs_t[heads + e:heads + e + 1, :]
            dec_ls = (jnp.exp(jnp.where(low, seg_f, NEG_BIG)) * dt_t[e:e + 1, :]
                      + jnp.exp(jnp.where(up, seg_b, NEG_BIG)) * dt_t[heads + e:heads + e + 1, :])
            part = _dot((cb * dec_ls).astype(BF16), halves[half])
            yb = part if yb is None else yb + part
        ys.append(yb)
    out["y_diag"] = jnp.concatenate(ys, axis=1)
    out["cm"] = cmb
    return out


def _ssd_kernel(z_ref, xs_ref, b_ref, c_ref, dt_ref, xsc_ref, bc_ref, dtc_ref,
                cwx_ref, cwb_ref, cwc_ref, cbx_ref, cbb_ref, cbc_ref,
                dtb_ref, alog_ref, dsk_ref, nw_ref, o_ref,
                y_scr, eab_scr, stb_scr, decb_scr, cm_scr, sf_scr, sb_scr, *, nc, ncc):
    hp = xs_ref.shape[-1]
    heads = hp // SSM_HEAD_DIM
    r2 = lax.broadcasted_iota(jnp.int32, (2 * CHUNK, CHUNK), 0)
    c2 = lax.broadcasted_iota(jnp.int32, (2 * CHUNK, CHUNK), 1)
    tri = jnp.where(((r2 < CHUNK) & (c2 <= r2)) | ((r2 >= CHUNK) & (c2 >= r2 - CHUNK)), 1.0, 0.0).astype(BF16)
    er = lax.broadcasted_iota(jnp.int32, (LANES, 2 * hp), 0)
    ec = lax.broadcasted_iota(jnp.int32, (LANES, 2 * hp), 1)
    expand = jnp.where(er == ec // SSM_HEAD_DIM, 1.0, 0.0).astype(BF16)
    consts = (tri, expand)
    cw = (cwx_ref, cwb_ref, cwc_ref)
    cbias = (cbx_ref, cbb_ref, cbc_ref)

    ctx = [_ssd_chunk(xsc_ref, bc_ref, None, dtc_ref, cw, cbias, dtb_ref, alog_ref, consts, c, ncc, False)
           for c in range(ncc)]
    s_f = jnp.zeros((D_STATE, hp), F32)
    for c in range(ncc):
        s_f = ctx[c]["dec_f"] * s_f + ctx[c]["st_f"]
    s_b = jnp.zeros((D_STATE, hp), F32)
    for c in reversed(range(ncc)):
        s_b = ctx[c]["dec_b"] * s_b + ctx[c]["st_b"]
    sf_scr[...] = s_f
    sb_scr[...] = s_b

    def fwd(c, carry):
        q = _ssd_chunk(xs_ref, b_ref, c_ref, dt_ref, cw, cbias, dtb_ref, alog_ref, consts, c, nc, True)
        base = pl.multiple_of(c * CHUNK, CHUNK)
        s_prev = sf_scr[...]
        y = q["y_diag"] + dsk_ref[...] * q["xs"] + _dot(q["cm"], s_prev.astype(BF16)) * q["ea_f"]
        sf_scr[...] = q["dec_f"] * s_prev + q["st_f"]
        y_scr[pl.ds(base, CHUNK), :] = y
        eab_scr[pl.ds(base, CHUNK), :] = q["ea_b"]
        cm_scr[pl.ds(base, CHUNK), :] = q["cm"]
        stb_scr[c] = q["st_b"]
        decb_scr[c] = jnp.broadcast_to(q["dec_b"], (8, hp))
        return carry

    lax.fori_loop(0, nc, fwd, 0)

    def bwd(i, carry):
        c = nc - 1 - i
        base = pl.multiple_of(c * CHUNK, CHUNK)
        s_prev = sb_scr[...]
        y = y_scr[pl.ds(base, CHUNK), :] + _dot(cm_scr[pl.ds(base, CHUNK), :], s_prev.astype(BF16)) * eab_scr[pl.ds(base, CHUNK), :]
        sb_scr[...] = decb_scr[c][0:1, :] * s_prev + stb_scr[c]
        u = y * _silu(z_ref[0, pl.ds(base, CHUNK), :])
        u = u * lax.rsqrt(jnp.mean(u * u, axis=-1, keepdims=True) + EPS)
        o_ref[0, pl.ds(base, CHUNK), :] = (u * nw_ref[...]).astype(BF16)
        return carry

    lax.fori_loop(0, nc, bwd, 0)


def _ssd(zx, dt, zx_c, dt_c, conv_w, conv_b, dtb, alog, dskip, norm_w, *, d_ssm):
    b, t, _ = zx.shape
    l = zx_c.shape[1]
    g = SSM_GROUPS
    hp = d_ssm // g
    n = D_STATE
    nc, ncc = t // CHUNK, l // CHUNK
    zb = d_ssm // hp
    bb = 2 * d_ssm // n
    cb0 = bb + g
    kern = functools.partial(_ssd_kernel, nc=nc, ncc=ncc)
    seq = lambda width, off: pl.BlockSpec((1, t, width), lambda bi, gi: (bi, 0, off + gi))
    seqc = lambda width, off: pl.BlockSpec((1, l, width), lambda bi, gi: (bi, 0, off + gi))
    par = lambda rows, width, off: pl.BlockSpec((rows, width), lambda bi, gi: (0, off + gi))
    return pl.pallas_call(
        kern,
        grid=(b, g),
        in_specs=[
            seq(hp, 0), seq(hp, zb), seq(n, bb), seq(n, cb0), seq(LANES, 0),
            seqc(hp, zb), seqc(n, bb), seqc(LANES, 0),
            par(D_CONV, hp, 0), par(D_CONV, n, d_ssm // n), par(D_CONV, n, d_ssm // n + g),
            par(1, hp, 0), par(1, n, d_ssm // n), par(1, n, d_ssm // n + g),
            pl.BlockSpec((1, 1, LANES), lambda bi, gi: (gi, 0, 0)),
            pl.BlockSpec((1, 1, LANES), lambda bi, gi: (gi, 0, 0)),
            par(1, hp, 0), par(1, hp, 0),
        ],
        out_specs=pl.BlockSpec((1, t, hp), lambda bi, gi: (bi, 0, gi)),
        out_shape=jax.ShapeDtypeStruct((b, t, d_ssm), BF16),
        scratch_shapes=[
            pltpu.VMEM((t, hp), F32),
            pltpu.VMEM((t, hp), F32),
            pltpu.VMEM((nc, n, hp), F32),
            pltpu.VMEM((nc, 8, hp), F32),
            pltpu.VMEM((t, n), BF16),
            pltpu.VMEM((n, hp), F32),
            pltpu.VMEM((n, hp), F32),
        ],
        compiler_params=_params(("arbitrary", "arbitrary"), 56),
    )(zx, zx, zx, zx, dt, zx_c, zx_c, dt_c,
      conv_w, conv_w, conv_w, conv_b, conv_b, conv_b, dtb, alog, dskip, norm_w)


def _outproj_kernel(a_ref, s_ref, x_ref, g1_ref, wa_ref, ws_ref, nw_ref, sh_ref, sc_ref, wr_ref,
                    h_ref, u_ref, aff_ref, *, n_exp):
    mix = _dot(a_ref[0], wa_ref[...]) + _dot(s_ref[0], ws_ref[...])
    h = x_ref[0] + g1_ref[0] * mix
    h_ref[0] = h
    y = h * lax.rsqrt(jnp.mean(h * h, axis=-1, keepdims=True) + EPS) * nw_ref[...]
    ub = (y * (1.0 + sc_ref[0]) + sh_ref[0]).astype(BF16)
    u_ref[0] = ub
    logits = _dot(ub, wr_ref[...])
    lane = lax.broadcasted_iota(jnp.int32, logits.shape, 1)
    logits = jnp.where(lane < n_exp, logits, NEG_BIG)
    p = jnp.exp(logits - jnp.max(logits, axis=-1, keepdims=True))
    aff = p / jnp.sum(p, axis=-1, keepdims=True)
    aff_ref[0] = aff.T[:n_exp, :]


def _out_proj(attn, ssm, x, g1, w_a, w_s, norm_w, shift, scale, w_r, *, n_exp, tm):
    b, t, d = x.shape
    da, ds = attn.shape[-1], ssm.shape[-1]
    kern = functools.partial(_outproj_kernel, n_exp=n_exp)
    vec = pl.BlockSpec((1, 1, d), lambda bi, i: (bi, 0, 0))
    return pl.pallas_call(
        kern,
        grid=(b, t // tm),
        in_specs=[
            pl.BlockSpec((1, tm, da), lambda bi, i: (bi, i, 0)),
            pl.BlockSpec((1, tm, ds), lambda bi, i: (bi, i, 0)),
            pl.BlockSpec((1, tm, d), lambda bi, i: (bi, i, 0)),
            vec,
            pl.BlockSpec((da, d), lambda bi, i: (0, 0)),
            pl.BlockSpec((ds, d), lambda bi, i: (0, 0)),
            pl.BlockSpec((1, d), lambda bi, i: (0, 0)),
            vec, vec,
            pl.BlockSpec((d, LANES), lambda bi, i: (0, 0)),
        ],
        out_specs=[
            pl.BlockSpec((1, tm, d), lambda bi, i: (bi, i, 0)),
            pl.BlockSpec((1, tm, d), lambda bi, i: (bi, i, 0)),
            pl.BlockSpec((1, n_exp, tm), lambda bi, i: (bi, 0, i)),
        ],
        out_shape=[
            jax.ShapeDtypeStruct((b, t, d), F32),
            jax.ShapeDtypeStruct((b, t, d), BF16),
            jax.ShapeDtypeStruct((b, n_exp, t), F32),
        ],
        compiler_params=_params(("arbitrary", "arbitrary"), 48),
    )(attn, ssm, x, g1, w_a, w_s, norm_w.reshape(1, d), shift, scale, w_r)


def _topk_kernel(aff_ref, pos_ref, *, cap):
    a = aff_ref[0]
    e, t = a.shape
    key = pltpu.bitcast(a, jnp.int32)
    count = lambda m: jnp.sum(jnp.where(m, 1.0, 0.0), axis=-1, keepdims=True)
    thr = jnp.zeros((e, 1), jnp.int32)
    for bit in range(30, -1, -1):
        cand = thr | (1 << bit)
        thr = jnp.where(count(key >= cand) >= cap, cand, thr)
    gt = key > thr
    eq = key == thr
    need = cap - count(gt)
    tok = lax.broadcasted_iota(jnp.int32, (e, t), 1)
    last = jnp.zeros((e, 1), jnp.int32)
    for bit in range(max(t - 1, 1).bit_length() - 1, -1, -1):
        cand = last | (1 << bit)
        last = jnp.where(count(eq & (tok < cand)) < need, cand, last)
    sel = gt | (eq & (tok <= last))
    selb = jnp.where(sel, 1.0, 0.0).astype(BF16)
    wb = min(t, 512)
    for j in range(t // wb):
        r = lax.broadcasted_iota(jnp.int32, (t, wb), 0)
        c = lax.broadcasted_iota(jnp.int32, (t, wb), 1) + j * wb
        slot = _dot(selb, jnp.where(r < c, 1.0, 0.0).astype(BF16)).astype(jnp.int32)
        pos_ref[0, :, j * wb:(j + 1) * wb] = jnp.where(sel[:, j * wb:(j + 1) * wb], slot, -1)


def _topk(aff, cap):
    b, e, t = aff.shape
    return pl.pallas_call(
        functools.partial(_topk_kernel, cap=cap),
        grid=(b,),
        in_specs=[pl.BlockSpec((1, e, t), lambda bi: (bi, 0, 0))],
        out_specs=pl.BlockSpec((1, e, t), lambda bi: (bi, 0, 0)),
        out_shape=jax.ShapeDtypeStruct((b, e, t), jnp.int32),
        compiler_params=_params(("arbitrary",), 48),
    )(aff)


def _gather_kernel(pos_ref, aff_ref, u_ref, xg_ref, gate_ref, *, cap):
    pos = pos_ref[0, 0]
    t = pos.shape[-1]
    slot = lax.broadcasted_iota(jnp.int32, (cap, t), 0)
    hit = pos == slot
    xg_ref[0] = _dot(jnp.where(hit, 1.0, 0.0).astype(BF16), u_ref[0]).astype(BF16)
    gate = jnp.sum(jnp.where(hit, aff_ref[0, 0], 0.0), axis=-1, keepdims=True)
    gate_ref[0] = jnp.broadcast_to(gate, (cap, LANES))


def _gather(pos, aff, u2, cap):
    b, e, t = pos.shape
    d = u2.shape[-1]
    row = pl.BlockSpec((1, 1, 1, t), lambda bi, ei: (bi, ei, 0, 0))
    return pl.pallas_call(
        functools.partial(_gather_kernel, cap=cap),
        grid=(b, e),
        in_specs=[row, row, pl.BlockSpec((1, t, d), lambda bi, ei: (bi, 0, 0))],
        out_specs=[
            pl.BlockSpec((1, cap, d), lambda bi, ei: (ei, bi, 0)),
            pl.BlockSpec((1, cap, LANES), lambda bi, ei: (ei, bi, 0)),
        ],
        out_shape=[
            jax.ShapeDtypeStruct((e, b * cap, d), BF16),
            jax.ShapeDtypeStruct((e, b * cap, LANES), F32),
        ],
        compiler_params=_params(("arbitrary", "arbitrary"), 48),
    )(pos.reshape(b, e, 1, t), aff.reshape(b, e, 1, t), u2)


def _moe_kernel(xg_ref, gate_ref, wg_ref, wu_ref, wd_ref, o_ref, acc_ref):
    f = pl.program_id(2)
    x = xg_ref[0]
    hg = _dot(x, wg_ref[0].astype(BF16))
    hu = _dot(x, wu_ref[0].astype(BF16))
    part = _dot((_silu(hg) * hu).astype(BF16), wd_ref[0].astype(BF16))

    @pl.when(f == 0)
    def _():
        acc_ref[...] = part

    @pl.when(f > 0)
    def _():
        acc_ref[...] += part

    @pl.when(f == pl.num_programs(2) - 1)
    def _():
        o_ref[0] = (acc_ref[...] * gate_ref[0][:, 0:1]).astype(BF16)


def _moe(xg, gate, w_gate, w_up, w_down, *, tm, fc):
    e, rows, d = xg.shape
    f = w_gate.shape[-1]
    return pl.pallas_call(
        _moe_kernel,
        grid=(e, rows // tm, f // fc),
        in_specs=[
            pl.BlockSpec((1, tm, d), lambda ei, i, fi: (ei, i, 0)),
            pl.BlockSpec((1, tm, LANES), lambda ei, i, fi: (ei, i, 0)),
            pl.BlockSpec((1, d, fc), lambda ei, i, fi: (ei, 0, fi)),
            pl.BlockSpec((1, d, fc), lambda ei, i, fi: (ei, 0, fi)),
            pl.BlockSpec((1, fc, d), lambda ei, i, fi: (ei, fi, 0)),
        ],
        out_specs=pl.BlockSpec((1, tm, d), lambda ei, i, fi: (ei, i, 0)),
        out_shape=jax.ShapeDtypeStruct((e, rows, d), BF16),
        scratch_shapes=[pltpu.VMEM((tm, d), F32)],
        compiler_params=_params(("arbitrary", "arbitrary", "arbitrary"), 56),
    )(xg, gate, w_gate, w_up, w_down)


def _combine_kernel(pos_ref, zg_ref, h_ref, g2_ref, fw_ref, o_ref, *, cap):
    pos = pos_ref[0]
    tb, e = pos.shape
    slot = lax.broadcasted_iota(jnp.int32, (tb, cap), 1)
    hot = jnp.concatenate(
        [jnp.where(pos[:, i:i + 1] == slot, 1.0, 0.0).astype(BF16) for i in range(e)], axis=1)
    z = zg_ref[...]
    ffn = _dot(hot, z.reshape(e * cap, z.shape[-1]))
    h = h_ref[0] + g2_ref[0] * ffn
    o_ref[0] = h * lax.rsqrt(jnp.mean(h * h, axis=-1, keepdims=True) + EPS) * fw_ref[...]


def _combine(pos_t, zg, h1, g2, final_w, *, cap, tb):
    b, t, e = pos_t.shape
    d = h1.shape[-1]
    return pl.pallas_call(
        functools.partial(_combine_kernel, cap=cap),
        grid=(b, t // tb),
        in_specs=[
            pl.BlockSpec((1, tb, e), lambda bi, i: (bi, i, 0)),
            pl.BlockSpec((e, cap, d), lambda bi, i: (0, bi, 0)),
            pl.BlockSpec((1, tb, d), lambda bi, i: (bi, i, 0)),
            pl.BlockSpec((1, 1, d), lambda bi, i: (bi, 0, 0)),
            pl.BlockSpec((1, d), lambda bi, i: (0, 0)),
        ],
        out_specs=pl.BlockSpec((1, tb, d), lambda bi, i: (bi, i, 0)),
        out_shape=jax.ShapeDtypeStruct((b, t, d), F32),
        compiler_params=_params(("arbitrary", "arbitrary"), 56),
    )(pos_t, zg, h1, g2, final_w.reshape(1, d))


def _layer(h_lat, h_ctx, c, c_ctx, w_ada, b_ada, norm_mix_w, norm_ffn_w, w_in, rpb, conv_w, conv_b,
           dt_bias, a_log, d_skip, ssm_norm_w, w_out, w_router, w_gate, w_up, w_down, final_norm_w):
    b, t, d = h_lat.shape
    l = h_ctx.shape[1]
    d_na = d // 2
    d_ssm = d // 2
    nh = d_na // HEAD_DIM_NA
    n_heads_ssm = d_ssm // SSM_HEAD_DIM
    hpg = n_heads_ssm // SSM_GROUPS
    conv_dim = d_ssm + 2 * SSM_GROUPS * D_STATE
    n_exp = w_router.shape[-1]
    cap = max(1, CAPACITY_FACTOR * t // n_exp)

    m_pad = -(-(b + 1) // 8) * 8
    cond = jnp.zeros((m_pad, d), F32).at[:b].set(c).at[b].set(c_ctx)
    mod = _ada_modulation(cond, w_ada, b_ada)
    sh1, sc1, g1, sh2, sc2, g2 = [mod[:b, i * d:(i + 1) * d].reshape(b, 1, d) for i in range(6)]
    csh1, csc1 = [jnp.broadcast_to(mod[b, i * d:(i + 1) * d].reshape(1, 1, d), (b, 1, d)) for i in range(2)]

    n_main = 3 * d_na + d_ssm + conv_dim
    w_main = w_in[:, :n_main].astype(BF16)
    w_dtr = w_in[:, n_main:].reshape(d, 2, SSM_GROUPS, hpg)
    w_dt = jnp.zeros((d, SSM_GROUPS, LANES), F32).at[:, :, :hpg].set(w_dtr[:, 0]).at[:, :, hpg:2 * hpg].set(w_dtr[:, 1])
    w_dt = w_dt.reshape(d, SSM_GROUPS * LANES).astype(BF16)

    def per_group_lanes(v):
        v = v.reshape(2, SSM_GROUPS, hpg)
        out = jnp.zeros((SSM_GROUPS, LANES), F32).at[:, :hpg].set(v[0]).at[:, hpg:2 * hpg].set(v[1])
        return out.reshape(SSM_GROUPS, 1, LANES)

    rope_lat = _rope_tables(t)
    rope_ctx = tuple(jnp.zeros((l, LANES), F32) for _ in range(3))
    qkv, zx, dt = _in_proj(h_lat, norm_mix_w, sh1, sc1, rope_lat, w_main, w_dt, d_na=d_na, rope=True,
                           tm=min(t, 1024))
    qkv_c, zx_c, dt_c = _in_proj(h_ctx, norm_mix_w, csh1, csc1, rope_ctx, w_main, w_dt, d_na=d_na, rope=False,
                                 tm=min(l, 1024))

    attn = _attention(qkv, qkv_c, _attention_bias(rpb), nh=nh)

    ssm = _ssd(zx, dt, zx_c, dt_c, conv_w, conv_b.reshape(1, conv_dim), per_group_lanes(dt_bias),
               per_group_lanes(a_log), jnp.repeat(d_skip, SSM_HEAD_DIM).reshape(1, d_ssm),
               ssm_norm_w.reshape(1, d_ssm), d_ssm=d_ssm)

    w_out_b = w_out.astype(BF16)
    w_r = jnp.zeros((d, LANES), BF16).at[:, :n_exp].set(w_router.astype(BF16))
    h1, u2, aff = _out_proj(attn, ssm, h_lat, g1, w_out_b[:d_na], w_out_b[d_na:], norm_ffn_w, sh2, sc2, w_r,
                            n_exp=n_exp, tm=min(t, 512))

    pos = _topk(aff, cap)
    xg, gate = _gather(pos, aff, u2, cap)
    zg = _moe(xg, gate, w_gate, w_up, w_down, tm=min(b * cap, 1024), fc=256)
    return _combine(jnp.swapaxes(pos, 1, 2), zg, h1, g2, final_norm_w, cap=cap, tb=min(t, 256))


def kernel(x, c, ctx, c_ctx, w_ada, b_ada, norm_mix_w, norm_ffn_w, w_in, rpb, conv_w, conv_b, dt_bias, a_log,
           d_skip, ssm_norm_w, w_out, w_router, w_gate, w_up, w_down, final_norm_w):
    assert w_ada.shape[0] == 1, "single-layer block"
    return _layer(x, ctx, c, c_ctx, w_ada[0], b_ada[0], norm_mix_w[0], norm_ffn_w[0], w_in[0], rpb[0],
                  conv_w[0], conv_b[0], dt_bias[0], a_log[0], d_skip[0], ssm_norm_w[0], w_out[0], w_router[0],
                  w_gate[0], w_up[0], w_down[0], final_norm_w)
```
